```python
import jax, jax.numpy as jnp
from jax import lax
import numpy as np

D_MODEL = 1024
BATCH = 4
SEQ = 8192
DEPTH = 2

CHUNK = 64
HEAD_DIM = 64
ATT_HEADS = 6
ATT_WIDTH = ATT_HEADS * HEAD_DIM
ATT_LEFT_CHUNKS = 8
ATT_BAND = (ATT_LEFT_CHUNKS + 1) * CHUNK
REL_CLIP = 128
CONV_WIDTH = D_MODEL // 4
CONV_KERNEL = 31
RET_HEADS = 6
RET_WIDTH = RET_HEADS * HEAD_DIM
D_MIX = ATT_WIDTH + CONV_WIDTH + RET_WIDTH
D_IN = 3 * ATT_WIDTH + 2 * CONV_WIDTH + 4 * RET_WIDTH
SPLITS = (ATT_WIDTH, 2 * ATT_WIDTH, 3 * ATT_WIDTH,
          3 * ATT_WIDTH + 2 * CONV_WIDTH,
          3 * ATT_WIDTH + 2 * CONV_WIDTH + RET_WIDTH,
          3 * ATT_WIDTH + 2 * CONV_WIDTH + 2 * RET_WIDTH,
          3 * ATT_WIDTH + 2 * CONV_WIDTH + 3 * RET_WIDTH)
D_FF = 4 * D_MODEL
ROPE_BASE = 10000.0
EPS = 1e-6
NEG_INF = -1e30

kernel_name = "hybrid_chunk_attn_conv_retention_encoder"


def rms_norm(x, g):
    xf = x.astype(jnp.float32)
    y = xf * lax.rsqrt(jnp.mean(xf * xf, axis=-1, keepdims=True) + EPS)
    return (y * g.astype(jnp.float32)).astype(x.dtype)


def layer_norm(x):
    xf = x.astype(jnp.float32)
    mu = jnp.mean(xf, axis=-1, keepdims=True)
    var = jnp.mean(jnp.square(xf - mu), axis=-1, keepdims=True)
    return (xf - mu) * lax.rsqrt(var + EPS)


def chunk_attention(q, k, v, rel_bias):
    b, s, _ = q.shape
    nc = s // CHUNK
    shp = (b, nc, CHUNK, ATT_HEADS, HEAD_DIM)
    q = q.reshape(shp) * (HEAD_DIM ** -0.5)
    k = k.reshape(shp)
    v = v.reshape(shp)
    pad = ((0, 0), (ATT_LEFT_CHUNKS, 0), (0, 0), (0, 0), (0, 0))
    kp, vp = jnp.pad(k, pad), jnp.pad(v, pad)
    band_idx = jnp.arange(nc)[:, None] + jnp.arange(ATT_LEFT_CHUNKS + 1)[None, :]
    kb = kp[:, band_idx].reshape(b, nc, ATT_BAND, ATT_HEADS, HEAD_DIM)
    vb = vp[:, band_idx].reshape(b, nc, ATT_BAND, ATT_HEADS, HEAD_DIM)
    scores = jnp.einsum('bnqhd,bnkhd->bnhqk', q, kb).astype(jnp.float32)
    qpos = jnp.arange(CHUNK)
    kpos = jnp.arange(ATT_BAND)
    rel = qpos[:, None] + ATT_LEFT_CHUNKS * CHUNK - kpos[None, :]
    rel_idx = jnp.clip(rel, -REL_CLIP, REL_CLIP) + REL_CLIP
    bias = rel_bias[:, rel_idx].astype(jnp.float32)
    key_chunk = jnp.arange(nc)[:, None] + (kpos // CHUNK)[None, :] - ATT_LEFT_CHUNKS
    valid = key_chunk >= 0
    scores = jnp.where(valid[None, :, None, None, :], scores + bias[None, None], NEG_INF)
    p = jax.nn.softmax(scores, axis=-1).astype(v.dtype)
    o = jnp.einsum('bnhqk,bnkhd->bnqhd', p, vb)
    return o.reshape(b, s, ATT_WIDTH)


def conv_module(u, conv_w, conv_b, ln_g, ln_b):
    a, gate = jnp.split(u, 2, axis=-1)
    y = a * jax.nn.sigmoid(gate)
    y = jnp.pad(y, ((0, 0), (CONV_KERNEL - 1, 0), (0, 0)))
    y = lax.conv_general_dilated(y, conv_w[:, None, :], (1,), 'VALID',
                                 dimension_numbers=('NWC', 'WIO', 'NWC'),
                                 feature_group_count=CONV_WIDTH) + conv_b
    y = layer_norm(y) * ln_g.astype(jnp.float32) + ln_b.astype(jnp.float32)
    return jax.nn.silu(y).astype(u.dtype)


def rotary(x, pos):
    half = HEAD_DIM // 2
    inv = 1.0 / (ROPE_BASE ** jnp.linspace(0.0, 1.0, half, dtype=jnp.float32))
    ang = pos[:, None] * inv[None, :]
    cos = jnp.cos(ang)[None, :, None, :]
    sin = jnp.sin(ang)[None, :, None, :]
    x1 = x[..., :half].astype(jnp.float32)
    x2 = x[..., half:].astype(jnp.float32)
    out = jnp.concatenate([x1 * cos - x2 * sin, x2 * cos + x1 * sin], axis=-1)
    return out.astype(x.dtype)


def retention(q, k, v, g):
    b, s, _ = q.shape
    nc = s // CHUNK
    dt = q.dtype
    pos = jnp.arange(s, dtype=jnp.float32)
    q = rotary(q.reshape(b, s, RET_HEADS, HEAD_DIM), pos)
    k = rotary(k.reshape(b, s, RET_HEADS, HEAD_DIM), pos) * (HEAD_DIM ** -0.5)
    shp = (b, nc, CHUNK, RET_HEADS, HEAD_DIM)
    qc, kc, vc = q.reshape(shp), k.reshape(shp), v.reshape(shp)
    log_g = jnp.log1p(-(2.0 ** (-5.0 - jnp.arange(RET_HEADS, dtype=jnp.float32))))
    idx = jnp.arange(CHUNK, dtype=jnp.float32)
    diff = idx[:, None] - idx[None, :]
    intra = jnp.where(diff >= 0, jnp.exp(jnp.maximum(diff, 0.0) * log_g[:, None, None]), 0.0)
    xi = jnp.exp((idx + 1.0)[None, :] * log_g[:, None]).astype(dt)
    zeta = jnp.exp((CHUNK - 1.0 - idx)[None, :] * log_g[:, None]).astype(dt)
    gamma_c = jnp.exp(CHUNK * log_g).astype(dt)
    sc = jnp.einsum('bnihd,bnjhd->bnhij', qc, kc) * intra.astype(dt)
    y_intra = jnp.einsum('bnhij,bnjhe->bnihe', sc, vc)
    kv = jnp.einsum('bnjhd,hj,bnjhe->nbhde', kc, zeta, vc)

    def step(state, kv_n):
        return state * gamma_c[None, :, None, None] + kv_n, state

    _, state_prev = lax.scan(step, jnp.zeros_like(kv[0]), kv)
    y_cross = jnp.einsum('bnihd,hi,nbhde->bnihe', qc, xi, state_prev)
    y = (y_intra + y_cross).reshape(b, s, RET_HEADS, HEAD_DIM)
    y = layer_norm(y).reshape(b, s, RET_WIDTH)
    return (jax.nn.silu(g.astype(jnp.float32)) * y).astype(dt)


def hybrid_layer(x, g_mix_pre, g_mix_post, g_mlp_pre, g_mlp_post, w_in, rel_bias,
                 conv_w, conv_b, conv_ln_g, conv_ln_b, w_out, w_up, w_down):
    h = rms_norm(x, g_mix_pre)
    u = h @ w_in
    aq, ak, av, cu, rq, rk, rv, rg = jnp.split(u, SPLITS, axis=-1)
    y_att = chunk_attention(aq, ak, av, rel_bias)
    y_conv = conv_module(cu, conv_w, conv_b, conv_ln_g, conv_ln_b)
    y_ret = retention(rq, rk, rv, rg)
    mix = jnp.concatenate([y_att, y_conv, y_ret], axis=-1) @ w_out
    x = x + rms_norm(mix, g_mix_post)
    h = rms_norm(x, g_mlp_pre)
    f = jnp.square(jax.nn.relu(h @ w_up)) @ w_down
    return x + rms_norm(f, g_mlp_post)


def setup_inputs(seed: int = 0) -> dict:
    key = jax.random.key(seed)
    ks = jax.random.split(key, 16)
    f32 = jnp.float32

    def nrm(k, shape, scale):
        return jax.random.normal(k, shape, f32) * scale

    def gain(k, shape):
        return 1.0 + nrm(k, shape, 0.02)

    return {
        "x": nrm(ks[0], (BATCH, SEQ, D_MODEL), 1.0),
        "norm_mix_pre": gain(ks[1], (DEPTH, D_MODEL)),
        "norm_mix_post": gain(ks[2], (DEPTH, D_MODEL)),
        "norm_mlp_pre": gain(ks[3], (DEPTH, D_MODEL)),
        "norm_mlp_post": gain(ks[4], (DEPTH, D_MODEL)),
        "w_in": nrm(ks[5], (DEPTH, D_MODEL, D_IN), D_MODEL ** -0.5),
        "attn_rel_bias": nrm(ks[6], (DEPTH, ATT_HEADS, 2 * REL_CLIP + 1), 0.5),
        "conv_w": nrm(ks[7], (DEPTH, CONV_KERNEL, CONV_WIDTH), CONV_KERNEL ** -0.5),
        "conv_b": nrm(ks[8], (DEPTH, CONV_WIDTH), 0.02),
        "conv_ln_g": gain(ks[9], (DEPTH, CONV_WIDTH)),
        "conv_ln_b": nrm(ks[10], (DEPTH, CONV_WIDTH), 0.02),
        "w_out": nrm(ks[11], (DEPTH, D_MIX, D_MODEL), D_MIX ** -0.5),
        "w_up": nrm(ks[12], (DEPTH, D_MODEL, D_FF), D_MODEL ** -0.5),
        "w_down": nrm(ks[13], (DEPTH, D_FF, D_MODEL), D_FF ** -0.5),
    }


def reference(x, norm_mix_pre, norm_mix_post, norm_mlp_pre, norm_mlp_post, w_in,
              attn_rel_bias, conv_w, conv_b, conv_ln_g, conv_ln_b, w_out, w_up, w_down):
    for l in range(DEPTH):
        x = hybrid_layer(x, norm_mix_pre[l], norm_mix_post[l], norm_mlp_pre[l], norm_mlp_post[l],
                         w_in[l], attn_rel_bias[l], conv_w[l], conv_b[l], conv_ln_g[l],
                         conv_ln_b[l], w_out[l], w_up[l], w_down[l])
    return x
```

```python
import functools

import jax
import jax.numpy as jnp
from jax import lax
from jax.experimental import pallas as pl
from jax.experimental.pallas import tpu as pltpu

D_MODEL = 1024
CHUNK = 64
HEAD_DIM = 64
HALF = HEAD_DIM // 2
ATT_HEADS = 6
ATT_WIDTH = ATT_HEADS * HEAD_DIM
ATT_LEFT_CHUNKS = 8
ATT_BAND = (ATT_LEFT_CHUNKS + 1) * CHUNK
REL_CLIP = 128
CONV_WIDTH = D_MODEL // 4
CONV_KERNEL = 31
RET_HEADS = 6
RET_WIDTH = RET_HEADS * HEAD_DIM
D_MIX = ATT_WIDTH + CONV_WIDTH + RET_WIDTH
D_FF = 4 * D_MODEL
ROPE_BASE = 10000.0
EPS = 1e-6
NEG_INF = -1e30
Q_SCALE = HEAD_DIM ** -0.5

COL_ATT = 0
COL_CONV = 3 * ATT_WIDTH
COL_RET = COL_CONV + 2 * CONV_WIDTH
COL_GATE = COL_RET + 3 * RET_WIDTH
D_IN = COL_GATE + RET_WIDTH

LANES = 128
VMEM_LIMIT = 56 * 1024 * 1024

TOK_TILE = 512
ATT_BLOCK = ATT_LEFT_CHUNKS * CHUNK
RET_BLOCK = 256
CONV_HALO = 32
CONV_ROWS = 64
FF_CHUNK = 1024

F32 = jnp.float32
BF16 = jnp.bfloat16


def _rms_norm(x, g):
    return x * lax.rsqrt(jnp.mean(x * x, axis=-1, keepdims=True) + EPS) * g


def _in_proj_kernel(x_ref, g_ref, w_ref, cos_ref, sin_ref, att_ref, conv_ref, ret_ref, gate_ref):
    h = _rms_norm(x_ref[...], g_ref[...]).astype(BF16)

    ua = jnp.dot(h, w_ref[:, COL_ATT:COL_CONV], preferred_element_type=F32)
    att_ref[:, :ATT_WIDTH] = (ua[:, :ATT_WIDTH] * Q_SCALE).astype(BF16)
    att_ref[:, ATT_WIDTH:] = ua[:, ATT_WIDTH:].astype(BF16)

    conv_ref[...] = jnp.dot(h, w_ref[:, COL_CONV:COL_RET], preferred_element_type=F32)

    ur = jnp.dot(h, w_ref[:, COL_RET:COL_GATE], preferred_element_type=F32)
    cos = cos_ref[...]
    sin = sin_ref[...]
    lane = lax.broadcasted_iota(jnp.int32, cos.shape, 1)
    first_half = (lane % HEAD_DIM) < HALF
    for blk in range(2 * RET_WIDTH // LANES):
        xb = ur[:, blk * LANES:(blk + 1) * LANES]
        partner = jnp.where(first_half,
                            pltpu.roll(xb, LANES - HALF, 1),
                            pltpu.roll(xb, HALF, 1))
        rot = xb * cos + partner * sin
        if blk * LANES >= RET_WIDTH:
            rot = rot * Q_SCALE
        ret_ref[:, blk * LANES:(blk + 1) * LANES] = rot.astype(BF16)
    ret_ref[:, 2 * RET_WIDTH:] = ur[:, 2 * RET_WIDTH:].astype(BF16)

    gate_ref[...] = jnp.dot(h, w_ref[:, COL_GATE:], preferred_element_type=F32)


def _in_proj(x, g, w, cos_t, sin_t):
    n = x.shape[0]
    seq_tiles = cos_t.shape[0] // TOK_TILE
    row = lambda i: (i, 0)
    const = lambda i: (0, 0)
    return pl.pallas_call(
        _in_proj_kernel,
        grid=(n // TOK_TILE,),
        in_specs=[
            pl.BlockSpec((TOK_TILE, D_MODEL), row),
            pl.BlockSpec((1, D_MODEL), const),
            pl.BlockSpec((D_MODEL, D_IN), const),
            pl.BlockSpec((TOK_TILE, LANES), lambda i: (i % seq_tiles, 0)),
            pl.BlockSpec((TOK_TILE, LANES), lambda i: (i % seq_tiles, 0)),
        ],
        out_specs=[
            pl.BlockSpec((TOK_TILE, 3 * ATT_WIDTH), row),
            pl.BlockSpec((TOK_TILE, 2 * CONV_WIDTH), row),
            pl.BlockSpec((TOK_TILE, 3 * RET_WIDTH), row),
            pl.BlockSpec((TOK_TILE, RET_WIDTH), row),
        ],
        out_shape=[
            jax.ShapeDtypeStruct((n, 3 * ATT_WIDTH), BF16),
            jax.ShapeDtypeStruct((n, 2 * CONV_WIDTH), F32),
            jax.ShapeDtypeStruct((n, 3 * RET_WIDTH), BF16),
            jax.ShapeDtypeStruct((n, RET_WIDTH), F32),
        ],
        compiler_params=pltpu.CompilerParams(
            dimension_semantics=("arbitrary",), vmem_limit_bytes=VMEM_LIMIT),
        name="in_proj",
    )(x, g, w, cos_t, sin_t)


def _attn_kernel(q_ref, kp_ref, kc_ref, vp_ref, vc_ref, bias_ref, o_ref, kcat, vcat):
    first_block = pl.program_id(1) == 0
    kcat[:ATT_BLOCK] = kp_ref[...]
    kcat[ATT_BLOCK:] = kc_ref[...]
    vcat[:ATT_BLOCK] = vp_ref[...]
    vcat[ATT_BLOCK:] = vc_ref[...]
    kidx = lax.broadcasted_iota(jnp.int32, (CHUNK, ATT_BAND), 1)

    def chunk_body(c, carry):
        row0 = pl.multiple_of(c * CHUNK, CHUNK)
        q = q_ref[pl.ds(row0, CHUNK), :]
        kb = kcat[pl.ds(row0, ATT_BAND), :]
        vb = vcat[pl.ds(row0, ATT_BAND), :]
        first_valid = jnp.where(first_block, ATT_BLOCK - c * CHUNK, 0)
        valid = kidx >= first_valid
        outs = []
        for h in range(ATT_HEADS):
            sl = slice(h * HEAD_DIM, (h + 1) * HEAD_DIM)
            s = lax.dot_general(q[:, sl], kb[:, sl], (((1,), (1,)), ((), ())),
                                preferred_element_type=F32)
            s = jnp.where(valid, s + bias_ref[h], NEG_INF)
            p = jnp.exp(s - jnp.max(s, axis=-1, keepdims=True))
            denom = jnp.sum(p, axis=-1, keepdims=True)
            o = jnp.dot(p.astype(BF16), vb[:, sl], preferred_element_type=F32)
            outs.append(o * (1.0 / denom))
        o_ref[pl.ds(row0, CHUNK), :] = jnp.concatenate(outs, axis=-1).astype(o_ref.dtype)
        return carry

    lax.fori_loop(0, ATT_BLOCK // CHUNK, chunk_body, 0)


def _attention(att, bias, batch):
    n = att.shape[0]
    nb = n // batch // ATT_BLOCK
    cur = lambda col: (lambda b, i: (b * nb + i, col))
    prev = lambda col: (lambda b, i: (b * nb + jnp.maximum(i - 1, 0), col))
    blk = (ATT_BLOCK, ATT_WIDTH)
    return pl.pallas_call(
        _attn_kernel,
        grid=(batch, nb),
        in_specs=[
            pl.BlockSpec(blk, cur(0)),
            pl.BlockSpec(blk, prev(1)),
            pl.BlockSpec(blk, cur(1)),
            pl.BlockSpec(blk, prev(2)),
            pl.BlockSpec(blk, cur(2)),
            pl.BlockSpec((ATT_HEADS, CHUNK, ATT_BAND), lambda b, i: (0, 0, 0)),
        ],
        out_specs=pl.BlockSpec(blk, cur(0)),
        out_shape=jax.ShapeDtypeStruct((n, ATT_WIDTH), BF16),
        scratch_shapes=[pltpu.VMEM((2 * ATT_BLOCK, ATT_WIDTH), BF16),
                        pltpu.VMEM((2 * ATT_BLOCK, ATT_WIDTH), BF16)],
        compiler_params=pltpu.CompilerParams(
            dimension_semantics=("arbitrary", "arbitrary"), vmem_limit_bytes=VMEM_LIMIT),
        name="attention",
    )(att, att, att, att, att, bias)


def _conv_kernel(halo_ref, cur_ref, w_ref, b_ref, lng_ref, lnb_ref, o_ref, ybuf, *, seq_tiles):
    first_tile = (pl.program_id(0) % seq_tiles) == 0
    halo = halo_ref[...]
    y_halo = halo[:, :CONV_WIDTH] * jax.nn.sigmoid(halo[:, CONV_WIDTH:])
    ybuf[:CONV_HALO] = jnp.where(first_tile, 0.0, y_halo)
    cur = cur_ref[...]
    ybuf[CONV_HALO:] = cur[:, :CONV_WIDTH] * jax.nn.sigmoid(cur[:, CONV_WIDTH:])
    lead = CONV_HALO - (CONV_KERNEL - 1)

    for row0 in range(0, TOK_TILE, CONV_ROWS):
        acc = jnp.zeros((CONV_ROWS, CONV_WIDTH), F32)
        for j in range(CONV_KERNEL):
            acc = acc + ybuf[row0 + lead + j:row0 + lead + j + CONV_ROWS, :] * w_ref[j:j + 1, :]
        y = acc + b_ref[...]
        mu = jnp.mean(y, axis=-1, keepdims=True)
        var = jnp.mean(jnp.square(y - mu), axis=-1, keepdims=True)
        y = (y - mu) * lax.rsqrt(var + EPS) * lng_ref[...] + lnb_ref[...]
        o_ref[row0:row0 + CONV_ROWS, :] = (y * jax.nn.sigmoid(y)).astype(o_ref.dtype)


def _conv(cu, w, b, ln_g, ln_b, seq):
    n = cu.shape[0]
    halo_per_tile = TOK_TILE // CONV_HALO
    const = lambda i: (0, 0)
    return pl.pallas_call(
        functools.partial(_conv_kernel, seq_tiles=seq // TOK_TILE),
        grid=(n // TOK_TILE,),
        in_specs=[
            pl.BlockSpec((CONV_HALO, 2 * CONV_WIDTH),
                         lambda i: (jnp.maximum(i * halo_per_tile - 1, 0), 0)),
            pl.BlockSpec((TOK_TILE, 2 * CONV_WIDTH), lambda i: (i, 0)),
            pl.BlockSpec((CONV_KERNEL, CONV_WIDTH), const),
            pl.BlockSpec((1, CONV_WIDTH), const),
            pl.BlockSpec((1, CONV_WIDTH), const),
            pl.BlockSpec((1, CONV_WIDTH), const),
        ],
        out_specs=pl.BlockSpec((TOK_TILE, CONV_WIDTH), lambda i: (i, 0)),
        out_shape=jax.ShapeDtypeStruct((n, CONV_WIDTH), BF16),
        scratch_shapes=[pltpu.VMEM((CONV_HALO + TOK_TILE, CONV_WIDTH), F32)],
        compiler_params=pltpu.CompilerParams(
            dimension_semantics=("arbitrary",), vmem_limit_bytes=VMEM_LIMIT),
        name="conv",
    )(cu, cu, w, b, ln_g, ln_b)


def _ret_kernel(q_ref, k_ref, v_ref, g_ref, decay_ref, xi_ref, zeta_ref, gamma_ref, o_ref, state):
    @pl.when(pl.program_id(1) == 0)
    def _():
        state[...] = jnp.zeros_like(state)

    q = q_ref[...]
    k = k_ref[...]
    v = v_ref[...]
    kz = (k.astype(F32) * zeta_ref[...]).astype(BF16)
    xi = xi_ref[...]
    gate = g_ref[...]
    outs = []
    for h in range(RET_HEADS):
        sl = slice(h * HEAD_DIM, (h + 1) * HEAD_DIM)
        sc = lax.dot_general(q[:, sl], k[:, sl], (((1,), (1,)), ((), ())),
                             preferred_element_type=F32) * decay_ref[h]
        y = jnp.dot(sc.astype(BF16), v[:, sl], preferred_element_type=F32)
        st = state[h]
        y = y + jnp.dot(q[:, sl], st.astype(BF16), preferred_element_type=F32) * xi[:, sl]
        kv = lax.dot_general(kz[:, sl], v[:, sl], (((0,), (0,)), ((), ())),
                             preferred_element_type=F32)
        state[h] = st * gamma_ref[h] + kv
        mu = jnp.mean(y, axis=-1, keepdims=True)
        var = jnp.mean(jnp.square(y - mu), axis=-1, keepdims=True)
        yn = (y - mu) * lax.rsqrt(var + EPS)
        gh = gate[:, sl]
        outs.append(gh * jax.nn.sigmoid(gh) * yn)
    o_ref[...] = jnp.concatenate(outs, axis=-1).astype(o_ref.dtype)


def _retention(ret, gate, decay, xi, zeta, gamma, batch):
    n = ret.shape[0]
    nb = n // batch // RET_BLOCK
    blk = (RET_BLOCK, RET_WIDTH)
    col = lambda c: (lambda b, t: (b * nb + t, c))
    const2 = lambda b, t: (0, 0)
    const3 = lambda b, t: (0, 0, 0)
    return pl.pallas_call(
        _ret_kernel,
        grid=(batch, nb),
        in_specs=[
            pl.BlockSpec(blk, col(0)),
            pl.BlockSpec(blk, col(1)),
            pl.BlockSpec(blk, col(2)),
            pl.BlockSpec(blk, col(0)),
            pl.BlockSpec((RET_HEADS, RET_BLOCK, RET_BLOCK), const3),
            pl.BlockSpec(blk, const2),
            pl.BlockSpec(blk, const2),
            pl.BlockSpec((RET_HEADS, HEAD_DIM, HEAD_DIM), const3),
        ],
        out_specs=pl.BlockSpec(blk, col(0)),
        out_shape=jax.ShapeDtypeStruct((n, RET_WIDTH), BF16),
        scratch_shapes=[pltpu.VMEM((RET_HEADS, HEAD_DIM, HEAD_DIM), F32)],
        compiler_params=pltpu.CompilerParams(
            dimension_semantics=("arbitrary", "arbitrary"), vmem_limit_bytes=VMEM_LIMIT),
        name="retention",
    )(ret, ret, ret, gate, decay, xi, zeta, gamma)


def _out_mlp_kernel(x_ref, ya_ref, yc_ref, yr_ref, wo_ref, wu_ref, wd_ref,
                    g_post_ref, g_pre2_ref, g_post2_ref, o_ref, mix_in, hid, acc):
    mix_in[:, :ATT_WIDTH] = ya_ref[...]
    mix_in[:, ATT_WIDTH:ATT_WIDTH + CONV_WIDTH] = yc_ref[...]
    mix_in[:, ATT_WIDTH + CONV_WIDTH:] = yr_ref[...]
    mix = jnp.dot(mix_in[...], wo_ref[...], preferred_element_type=F32)
    x1 = x_ref[...] + _rms_norm(mix, g_post_ref[...])
    hid[...] = _rms_norm(x1, g_pre2_ref[...]).astype(BF16)
    for c in range(D_FF // FF_CHUNK):
        cols = slice(c * FF_CHUNK, (c + 1) * FF_CHUNK)
        up = jnp.dot(hid[...], wu_ref[:, cols], preferred_element_type=F32)
        act = jnp.square(jnp.maximum(up, 0.0)).astype(BF16)
        part = jnp.dot(act, wd_ref[cols, :], preferred_element_type=F32)
        if c == 0:
            acc[...] = part
        else:
            acc[...] += part
    o_ref[...] = x1 + _rms_norm(acc[...], g_post2_ref[...])


def _out_mlp(x, y_att, y_conv, y_ret, w_out, w_up, w_down, g_post, g_pre2, g_post2):
    n = x.shape[0]
    row = lambda i: (i, 0)
    const = lambda i: (0, 0)
    resident = lambda shape: pl.BlockSpec(shape, const, pipeline_mode=pl.Buffered(1))
    gain = pl.BlockSpec((1, D_MODEL), const)
    return pl.pallas_call(
        _out_mlp_kernel,
        grid=(n // TOK_TILE,),
        in_specs=[
            pl.BlockSpec((TOK_TILE, D_MODEL), row),
            pl.BlockSpec((TOK_TILE, ATT_WIDTH), row),
            pl.BlockSpec((TOK_TILE, CONV_WIDTH), row),
            pl.BlockSpec((TOK_TILE, RET_WIDTH), row),
            resident((D_MIX, D_MODEL)),
            resident((D_MODEL, D_FF)),
            resident((D_FF, D_MODEL)),
            gain, gain, gain,
        ],
        out_specs=pl.BlockSpec((TOK_TILE, D_MODEL), row),
        out_shape=jax.ShapeDtypeStruct((n, D_MODEL), F32),
        scratch_shapes=[pltpu.VMEM((TOK_TILE, D_MIX), BF16),
                        pltpu.VMEM((TOK_TILE, D_MODEL), BF16),
                        pltpu.VMEM((TOK_TILE, D_MODEL), F32)],
        compiler_params=pltpu.CompilerParams(
            dimension_semantics=("arbitrary",), vmem_limit_bytes=VMEM_LIMIT),
        name="out_mlp",
    )(x, y_att, y_conv, y_ret, w_out, w_up, w_down, g_post, g_pre2, g_post2)


def _rotary_tables(seq):
    inv = 1.0 / (ROPE_BASE ** jnp.linspace(0.0, 1.0, HALF, dtype=F32))
    ang = jnp.arange(seq, dtype=F32)[:, None] * inv[None, :]
    cos = jnp.cos(ang)
    sin = jnp.sin(ang)
    reps = LANES // HEAD_DIM
    cos_t = jnp.tile(jnp.concatenate([cos, cos], axis=1), (1, reps))
    sin_t = jnp.tile(jnp.concatenate([-sin, sin], axis=1), (1, reps))
    return cos_t, sin_t


def _retention_tables():
    log_g = jnp.log1p(-(2.0 ** (-5.0 - jnp.arange(RET_HEADS, dtype=F32))))
    idx = jnp.arange(RET_BLOCK, dtype=F32)
    diff = idx[:, None] - idx[None, :]
    decay = jnp.where(diff >= 0, jnp.exp(jnp.maximum(diff, 0.0) * log_g[:, None, None]), 0.0)
    xi = jnp.exp((idx + 1.0)[None, :] * log_g[:, None])
    zeta = jnp.exp((RET_BLOCK - 1.0 - idx)[None, :] * log_g[:, None])
    gamma = jnp.exp(RET_BLOCK * log_g)
    widen = lambda t: jnp.repeat(t.T, HEAD_DIM, axis=1)
    gamma_t = jnp.broadcast_to(gamma[:, None, None], (RET_HEADS, HEAD_DIM, HEAD_DIM))
    return decay, widen(xi), widen(zeta), gamma_t


def _attention_bias(rel_bias):
    qpos = jnp.arange(CHUNK)
    kpos = jnp.arange(ATT_BAND)
    rel = qpos[:, None] + ATT_LEFT_CHUNKS * CHUNK - kpos[None, :]
    rel_idx = jnp.clip(rel, -REL_CLIP, REL_CLIP) + REL_CLIP
    return rel_bias[:, rel_idx]


def kernel(x, norm_mix_pre, norm_mix_post, norm_mlp_pre, norm_mlp_post, w_in, attn_rel_bias,
           conv_w, conv_b, conv_ln_g, conv_ln_b, w_out, w_up, w_down):
    batch, seq, d_model = x.shape
    depth = w_in.shape[0]
    assert d_model == D_MODEL and w_in.shape[2] == D_IN
    assert seq % TOK_TILE == 0 and seq % ATT_BLOCK == 0 and seq % RET_BLOCK == 0

    cos_t, sin_t = _rotary_tables(seq)
    decay, xi, zeta, gamma = _retention_tables()
    row = lambda v: v.reshape(1, -1)

    xf = x.reshape(batch * seq, d_model)
    for l in range(depth):
        att, cu, ret, gate = _in_proj(xf, row(norm_mix_pre[l]), w_in[l].astype(BF16), cos_t, sin_t)
        y_att = _attention(att, _attention_bias(attn_rel_bias[l]), batch)
        y_conv = _conv(cu, conv_w[l], row(conv_b[l]), row(conv_ln_g[l]), row(conv_ln_b[l]), seq)
        y_ret = _retention(ret, gate, decay, xi, zeta, gamma, batch)
        xf = _out_mlp(xf, y_att, y_conv, y_ret, w_out[l].astype(BF16), w_up[l].astype(BF16),
                      w_down[l].astype(BF16), row(norm_mix_post[l]), row(norm_mlp_pre[l]),
                      row(norm_mlp_post[l]))
    return xf.reshape(batch, seq, d_model)
```

```python
import functools

import jax
import jax.numpy as jnp
import numpy as np
from jax import lax
from jax.experimental import pallas as pl
from jax.experimental.pallas import tpu as pltpu

D_MODEL = 1024
CHUNK = 64
HEAD_DIM = 64
HALF = HEAD_DIM // 2
ATT_HEADS = 6
ATT_WIDTH = ATT_HEADS * HEAD_DIM
ATT_LEFT_CHUNKS = 8
ATT_BAND = (ATT_LEFT_CHUNKS + 1) * CHUNK
REL_CLIP = 128
CONV_WIDTH = D_MODEL // 4
CONV_KERNEL = 31
RET_HEADS = 6
RET_WIDTH = RET_HEADS * HEAD_DIM
D_MIX = ATT_WIDTH + CONV_WIDTH + RET_WIDTH
D_FF = 4 * D_MODEL
ROPE_BASE = 10000.0
EPS = 1e-6
NEG_INF = -1e30
Q_SCALE = HEAD_DIM ** -0.5

COL_ATT = 0
COL_CONV = 3 * ATT_WIDTH
COL_RET = COL_CONV + 2 * CONV_WIDTH
COL_GATE = COL_RET + 3 * RET_WIDTH
D_IN = COL_GATE + RET_WIDTH

LANES = 128
SUBLANES = 8
VMEM_LIMIT = 56 * 1024 * 1024

TOK_TILE = 512
ATT_BLOCK = ATT_LEFT_CHUNKS * CHUNK
PAIR = 2 * CHUNK
PAIR_BAND = ATT_BAND + CHUNK
RET_BLOCK = 256
CONV_HALO = 32
CONV_ROWS = 64
FF_CHUNK = 1024

F32 = jnp.float32
BF16 = jnp.bfloat16


def _rms_norm(x, g):
    return x * lax.rsqrt(jnp.mean(x * x, axis=-1, keepdims=True) + EPS) * g


def _in_proj_kernel(x_ref, g_ref, w_ref, cos_ref, sin_ref, att_ref, conv_ref, ret_ref, gate_ref):
    h = _rms_norm(x_ref[...], g_ref[...]).astype(BF16)

    ua = jnp.dot(h, w_ref[:, COL_ATT:COL_CONV], preferred_element_type=F32)
    att_ref[:, :ATT_WIDTH] = (ua[:, :ATT_WIDTH] * Q_SCALE).astype(BF16)
    att_ref[:, ATT_WIDTH:] = ua[:, ATT_WIDTH:].astype(BF16)

    conv_ref[...] = jnp.dot(h, w_ref[:, COL_CONV:COL_RET], preferred_element_type=F32)

    ur = jnp.dot(h, w_ref[:, COL_RET:COL_GATE], preferred_element_type=F32)
    cos = cos_ref[...]
    sin = sin_ref[...]
    lane = lax.broadcasted_iota(jnp.int32, cos.shape, 1)
    first_half = (lane % HEAD_DIM) < HALF
    for blk in range(2 * RET_WIDTH // LANES):
        xb = ur[:, blk * LANES:(blk + 1) * LANES]
        partner = jnp.where(first_half,
                            pltpu.roll(xb, LANES - HALF, 1),
                            pltpu.roll(xb, HALF, 1))
        rot = xb * cos + partner * sin
        if blk * LANES >= RET_WIDTH:
            rot = rot * Q_SCALE
        ret_ref[:, blk * LANES:(blk + 1) * LANES] = rot.astype(BF16)
    ret_ref[:, 2 * RET_WIDTH:] = ur[:, 2 * RET_WIDTH:].astype(BF16)

    gate_ref[...] = jnp.dot(h, w_ref[:, COL_GATE:], preferred_element_type=F32)


def _in_proj(x, g, w, cos_t, sin_t):
    n = x.shape[0]
    seq_tiles = cos_t.shape[0] // TOK_TILE
    row = lambda i: (i, 0)
    const = lambda i: (0, 0)
    return pl.pallas_call(
        _in_proj_kernel,
        grid=(n // TOK_TILE,),
        in_specs=[
            pl.BlockSpec((TOK_TILE, D_MODEL), row),
            pl.BlockSpec((1, D_MODEL), const),
            pl.BlockSpec((D_MODEL, D_IN), const),
            pl.BlockSpec((TOK_TILE, LANES), lambda i: (i % seq_tiles, 0)),
            pl.BlockSpec((TOK_TILE, LANES), lambda i: (i % seq_tiles, 0)),
        ],
        out_specs=[
            pl.BlockSpec((TOK_TILE, 3 * ATT_WIDTH), row),
            pl.BlockSpec((TOK_TILE, 2 * CONV_WIDTH), row),
            pl.BlockSpec((TOK_TILE, 3 * RET_WIDTH), row),
            pl.BlockSpec((TOK_TILE, RET_WIDTH), row),
        ],
        out_shape=[
            jax.ShapeDtypeStruct((n, 3 * ATT_WIDTH), BF16),
            jax.ShapeDtypeStruct((n, 2 * CONV_WIDTH), F32),
            jax.ShapeDtypeStruct((n, 3 * RET_WIDTH), BF16),
            jax.ShapeDtypeStruct((n, RET_WIDTH), F32),
        ],
        compiler_params=pltpu.CompilerParams(
            dimension_semantics=("arbitrary",), vmem_limit_bytes=VMEM_LIMIT),
        name="in_proj",
    )(x, g, w, cos_t, sin_t)


def _attn_kernel(q_ref, kp_ref, kc_ref, vp_ref, vc_ref, bias_ref, o_ref, v_t, s_buf):
    first_block = pl.program_id(1) == 0
    v_t[:, :ATT_BLOCK] = vp_ref[...].astype(F32).T.astype(BF16)
    v_t[:, ATT_BLOCK:] = vc_ref[...].astype(F32).T.astype(BF16)
    lane = lax.broadcasted_iota(jnp.int32, (PAIR, LANES), 1)
    low_head = lane < HEAD_DIM

    head_pairs = ATT_HEADS // 2
    n_units = (ATT_BLOCK // PAIR) * head_pairs

    def scores(u):
        pr, hp = divmod(u, head_pairs)
        row0 = pr * PAIR
        cols = slice(hp * LANES, (hp + 1) * LANES)
        variant = jnp.where(first_block, (1 + pr) * ATT_HEADS, 0)
        q2 = q_ref[row0:row0 + PAIR, cols]
        qz = jnp.concatenate([jnp.where(low_head, q2, jnp.zeros_like(q2)),
                              jnp.where(low_head, jnp.zeros_like(q2), q2)], axis=0)
        kb = jnp.concatenate([kp_ref[row0:, cols], kc_ref[:row0 + PAIR, cols]], axis=0)
        s_pair = lax.dot_general(kb, qz, (((1,), (1,)), ((), ())),
                                 preferred_element_type=F32)
        maxima = []
        for hh in range(2):
            s = s_pair[:, hh * PAIR:(hh + 1) * PAIR] + bias_ref[variant + 2 * hp + hh]
            s_buf[u % 2, :, hh * PAIR:(hh + 1) * PAIR] = s
            maxima.append(jnp.max(s, axis=0, keepdims=True))
        return maxima

    def outputs(u, maxima):
        pr, hp = divmod(u, head_pairs)
        row0 = pr * PAIR
        tiles = []
        for hh in range(2):
            h = 2 * hp + hh
            p = jnp.exp(s_buf[u % 2, :, hh * PAIR:(hh + 1) * PAIR] - maxima[hh])
            denom = jnp.sum(p, axis=0, keepdims=True)
            o_t = jnp.dot(v_t[h * HEAD_DIM:(h + 1) * HEAD_DIM, row0:row0 + PAIR_BAND],
                          p.astype(BF16), preferred_element_type=F32)
            tiles.append(o_t * (1.0 / denom))
        return tiles

    maxima = scores(0)
    out_t = []
    for u in range(n_units):
        next_maxima = scores(u + 1) if u + 1 < n_units else None
        out_t += outputs(u, maxima)
        maxima = next_maxima
        if (u + 1) % head_pairs == 0:
            row0 = (u // head_pairs) * PAIR
            o_ref[row0:row0 + PAIR, :] = jnp.concatenate(out_t, axis=0).T.astype(o_ref.dtype)
            out_t = []


def _attention(att, bias, batch):
    n = att.shape[0]
    nb = n // batch // ATT_BLOCK
    cur = lambda col: (lambda b, i: (b * nb + i, col))
    prev = lambda col: (lambda b, i: (b * nb + jnp.maximum(i - 1, 0), col))
    blk = (ATT_BLOCK, ATT_WIDTH)
    return pl.pallas_call(
        _attn_kernel,
        grid=(batch, nb),
        in_specs=[
            pl.BlockSpec(blk, cur(0)),
            pl.BlockSpec(blk, prev(1)),
            pl.BlockSpec(blk, cur(1)),
            pl.BlockSpec(blk, prev(2)),
            pl.BlockSpec(blk, cur(2)),
            pl.BlockSpec(bias.shape, lambda b, i: (0, 0, 0), pipeline_mode=pl.Buffered(1)),
        ],
        out_specs=pl.BlockSpec(blk, cur(0)),
        out_shape=jax.ShapeDtypeStruct((n, ATT_WIDTH), BF16),
        scratch_shapes=[pltpu.VMEM((ATT_WIDTH, 2 * ATT_BLOCK), BF16),
                        pltpu.VMEM((2, PAIR_BAND, 2 * PAIR), F32)],
        compiler_params=pltpu.CompilerParams(
            dimension_semantics=("arbitrary", "arbitrary"), vmem_limit_bytes=VMEM_LIMIT),
        name="attention",
    )(att, att, att, att, att, bias)


def _conv_kernel(halo_ref, cur_ref, w_ref, b_ref, lng_ref, lnb_ref, o_ref, ybuf, *, seq_tiles):
    first_tile = (pl.program_id(0) % seq_tiles) == 0
    halo = halo_ref[...]
    y_halo = halo[:, :CONV_WIDTH] * jax.nn.sigmoid(halo[:, CONV_WIDTH:])
    ybuf[0, :CONV_HALO] = jnp.where(first_tile, 0.0, y_halo)
    cur = cur_ref[...]
    ybuf[0, CONV_HALO:] = cur[:, :CONV_WIDTH] * jax.nn.sigmoid(cur[:, CONV_WIDTH:])
    lead = CONV_HALO - (CONV_KERNEL - 1)
    shifted_rows = CONV_HALO + TOK_TILE - SUBLANES
    for r in range(1, SUBLANES):
        ybuf[r, :shifted_rows] = ybuf[0, r:r + shifted_rows]

    for row0 in range(0, TOK_TILE, CONV_ROWS):
        acc = jnp.zeros((CONV_ROWS, CONV_WIDTH), F32)
        for j in range(CONV_KERNEL):
            r = (lead + j) % SUBLANES
            start = row0 + lead + j - r
            acc = acc + ybuf[r, start:start + CONV_ROWS] * w_ref[j:j + 1, :]
        y = acc + b_ref[...]
        mu = jnp.mean(y, axis=-1, keepdims=True)
        var = jnp.mean(jnp.square(y - mu), axis=-1, keepdims=True)
        y = (y - mu) * lax.rsqrt(var + EPS) * lng_ref[...] + lnb_ref[...]
        o_ref[row0:row0 + CONV_ROWS, :] = (y * jax.nn.sigmoid(y)).astype(o_ref.dtype)


def _conv(cu, w, b, ln_g, ln_b, seq):
    n = cu.shape[0]
    halo_per_tile = TOK_TILE // CONV_HALO
    const = lambda i: (0, 0)
    return pl.pallas_call(
        functools.partial(_conv_kernel, seq_tiles=seq // TOK_TILE),
        grid=(n // TOK_TILE,),
        in_specs=[
            pl.BlockSpec((CONV_HALO, 2 * CONV_WIDTH),
                         lambda i: (jnp.maximum(i * halo_per_tile - 1, 0), 0)),
            pl.BlockSpec((TOK_TILE, 2 * CONV_WIDTH), lambda i: (i, 0)),
            pl.BlockSpec((CONV_KERNEL, CONV_WIDTH), const),
            pl.BlockSpec((1, CONV_WIDTH), const),
            pl.BlockSpec((1, CONV_WIDTH), const),
            pl.BlockSpec((1, CONV_WIDTH), const),
        ],
        out_specs=pl.BlockSpec((TOK_TILE, CONV_WIDTH), lambda i: (i, 0)),
        out_shape=jax.ShapeDtypeStruct((n, CONV_WIDTH), BF16),
        scratch_shapes=[pltpu.VMEM((SUBLANES, CONV_HALO + TOK_TILE, CONV_WIDTH), F32)],
        compiler_params=pltpu.CompilerParams(
            dimension_semantics=("arbitrary",), vmem_limit_bytes=VMEM_LIMIT),
        name="conv",
    )(cu, cu, w, b, ln_g, ln_b)


def _ret_kernel(q_ref, k_ref, v_ref, g_ref, decay_ref, xi_ref, zeta_ref, gamma_ref, o_ref, state):
    @pl.when(pl.program_id(1) == 0)
    def _():
        state[...] = jnp.zeros_like(state)

    q = q_ref[...]
    k = k_ref[...]
    v = v_ref[...]
    kz = (k.astype(F32) * zeta_ref[...]).astype(BF16)
    xi = xi_ref[...]
    gate = g_ref[...]
    outs = []
    for h in range(RET_HEADS):
        sl = slice(h * HEAD_DIM, (h + 1) * HEAD_DIM)
        sc = lax.dot_general(q[:, sl], k[:, sl], (((1,), (1,)), ((), ())),
                             preferred_element_type=F32) * decay_ref[h]
        y = jnp.dot(sc.astype(BF16), v[:, sl], preferred_element_type=F32)
        st = state[h]
        y = y + jnp.dot(q[:, sl], st.astype(BF16), preferred_element_type=F32) * xi[:, sl]
        kv = lax.dot_general(kz[:, sl], v[:, sl], (((0,), (0,)), ((), ())),
                             preferred_element_type=F32)
        state[h] = st * gamma_ref[h] + kv
        mu = jnp.mean(y, axis=-1, keepdims=True)
        var = jnp.mean(jnp.square(y - mu), axis=-1, keepdims=True)
        yn = (y - mu) * lax.rsqrt(var + EPS)
        gh = gate[:, sl]
        outs.append(gh * jax.nn.sigmoid(gh) * yn)
    o_ref[...] = jnp.concatenate(outs, axis=-1).astype(o_ref.dtype)


def _retention(ret, gate, decay, xi, zeta, gamma, batch):
    n = ret.shape[0]
    nb = n // batch // RET_BLOCK
    blk = (RET_BLOCK, RET_WIDTH)
    col = lambda c: (lambda b, t: (b * nb + t, c))
    const2 = lambda b, t: (0, 0)
    const3 = lambda b, t: (0, 0, 0)
    return pl.pallas_call(
        _ret_kernel,
        grid=(batch, nb),
        in_specs=[
            pl.BlockSpec(blk, col(0)),
            pl.BlockSpec(blk, col(1)),
            pl.BlockSpec(blk, col(2)),
            pl.BlockSpec(blk, col(0)),
            pl.BlockSpec((RET_HEADS, RET_BLOCK, RET_BLOCK), const3),
            pl.BlockSpec(blk, const2),
            pl.BlockSpec(blk, const2),
            pl.BlockSpec((RET_HEADS, HEAD_DIM, HEAD_DIM), const3),
        ],
        out_specs=pl.BlockSpec(blk, col(0)),
        out_shape=jax.ShapeDtypeStruct((n, RET_WIDTH), BF16),
        scratch_shapes=[pltpu.VMEM((RET_HEADS, HEAD_DIM, HEAD_DIM), F32)],
        compiler_params=pltpu.CompilerParams(
            dimension_semantics=("arbitrary", "arbitrary"), vmem_limit_bytes=VMEM_LIMIT),
        name="retention",
    )(ret, ret, ret, gate, decay, xi, zeta, gamma)


def _out_mlp_kernel(x_ref, ya_ref, yc_ref, yr_ref, wo_ref, wu_ref, wd_ref,
                    g_post_ref, g_pre2_ref, g_post2_ref, o_ref, mix_in, hid, acc):
    mix_in[:, :ATT_WIDTH] = ya_ref[...]
    mix_in[:, ATT_WIDTH:ATT_WIDTH + CONV_WIDTH] = yc_ref[...]
    mix_in[:, ATT_WIDTH + CONV_WIDTH:] = yr_ref[...]
    mix = jnp.dot(mix_in[...], wo_ref[...], preferred_element_type=F32)
    x1 = x_ref[...] + _rms_norm(mix, g_post_ref[...])
    hid[...] = _rms_norm(x1, g_pre2_ref[...]).astype(BF16)
    for c in range(D_FF // FF_CHUNK):
        cols = slice(c * FF_CHUNK, (c + 1) * FF_CHUNK)
        up = jnp.dot(hid[...], wu_ref[:, cols], preferred_element_type=F32)
        act = jnp.square(jnp.maximum(up, 0.0)).astype(BF16)
        part = jnp.dot(act, wd_ref[cols, :], preferred_element_type=F32)
        if c == 0:
            acc[...] = part
        else:
            acc[...] += part
    o_ref[...] = x1 + _rms_norm(acc[...], g_post2_ref[...])


def _out_mlp(x, y_att, y_conv, y_ret, w_out, w_up, w_down, g_post, g_pre2, g_post2):
    n = x.shape[0]
    row = lambda i: (i, 0)
    const = lambda i: (0, 0)
    resident = lambda shape: pl.BlockSpec(shape, const, pipeline_mode=pl.Buffered(1))
    gain = pl.BlockSpec((1, D_MODEL), const)
    return pl.pallas_call(
        _out_mlp_kernel,
        grid=(n // TOK_TILE,),
        in_specs=[
            pl.BlockSpec((TOK_TILE, D_MODEL), row),
            pl.BlockSpec((TOK_TILE, ATT_WIDTH), row),
            pl.BlockSpec((TOK_TILE, CONV_WIDTH), row),
            pl.BlockSpec((TOK_TILE, RET_WIDTH), row),
            resident((D_MIX, D_MODEL)),
            resident((D_MODEL, D_FF)),
            resident((D_FF, D_MODEL)),
            gain, gain, gain,
        ],
        out_specs=pl.BlockSpec((TOK_TILE, D_MODEL), row),
        out_shape=jax.ShapeDtypeStruct((n, D_MODEL), F32),
        scratch_shapes=[pltpu.VMEM((TOK_TILE, D_MIX), BF16),
                        pltpu.VMEM((TOK_TILE, D_MODEL), BF16),
                        pltpu.VMEM((TOK_TILE, D_MODEL), F32)],
        compiler_params=pltpu.CompilerParams(
            dimension_semantics=("arbitrary",), vmem_limit_bytes=VMEM_LIMIT),
        name="out_mlp",
    )(x, y_att, y_conv, y_ret, w_out, w_up, w_down, g_post, g_pre2, g_post2)


def _rotary_tables(seq):
    inv = 1.0 / (ROPE_BASE ** jnp.linspace(0.0, 1.0, HALF, dtype=F32))
    ang = jnp.arange(seq, dtype=F32)[:, None] * inv[None, :]
    cos = jnp.cos(ang)
    sin = jnp.sin(ang)
    reps = LANES // HEAD_DIM
    cos_t = jnp.tile(jnp.concatenate([cos, cos], axis=1), (1, reps))
    sin_t = jnp.tile(jnp.concatenate([-sin, sin], axis=1), (1, reps))
    return cos_t, sin_t


def _retention_tables():
    log_g = jnp.log1p(-(2.0 ** (-5.0 - jnp.arange(RET_HEADS, dtype=F32))))
    idx = jnp.arange(RET_BLOCK, dtype=F32)
    diff = idx[:, None] - idx[None, :]
    decay = jnp.where(diff >= 0, jnp.exp(jnp.maximum(diff, 0.0) * log_g[:, None, None]), 0.0)
    xi = jnp.exp((idx + 1.0)[None, :] * log_g[:, None])
    zeta = jnp.exp((RET_BLOCK - 1.0 - idx)[None, :] * log_g[:, None])
    gamma = jnp.exp(RET_BLOCK * log_g)
    widen = lambda t: jnp.repeat(t.T, HEAD_DIM, axis=1)
    gamma_t = jnp.broadcast_to(gamma[:, None, None], (RET_HEADS, HEAD_DIM, HEAD_DIM))
    return decay, widen(xi), widen(zeta), gamma_t


def _attention_bias(rel_bias):
    left = ATT_LEFT_CHUNKS * CHUNK
    period = PAIR_BAND + PAIR
    m = np.arange(period)
    offset = np.where(m <= PAIR_BAND, m, m - period)
    rel_idx = np.clip(left - offset, -REL_CLIP, REL_CLIP) + REL_CLIP
    g_rev = rel_bias[:, rel_idx[(-m) % period]]
    flat = jnp.tile(g_rev, (1, PAIR_BAND))[:, :PAIR_BAND * (period - 1)]
    toeplitz = flat.reshape(ATT_HEADS, PAIR_BAND, period - 1)[:, :, :PAIR]
    r = np.arange(PAIR_BAND)[:, None]
    j = np.arange(PAIR)[None, :]
    valid = np.where(j < CHUNK, r < ATT_BAND, r >= CHUNK)
    n_prev = [0] + [ATT_BLOCK - pr * PAIR for pr in range(ATT_BLOCK // PAIR)]
    valid = np.stack([valid & (r >= n) for n in n_prev])[:, None]
    table = jnp.where(valid, toeplitz[None], NEG_INF)
    return table.reshape(len(n_prev) * ATT_HEADS, PAIR_BAND, PAIR)


def kernel(x, norm_mix_pre, norm_mix_post, norm_mlp_pre, norm_mlp_post, w_in, attn_rel_bias,
           conv_w, conv_b, conv_ln_g, conv_ln_b, w_out, w_up, w_down):
    batch, seq, d_model = x.shape
    depth = w_in.shape[0]
    assert d_model == D_MODEL and w_in.shape[2] == D_IN
    assert seq % TOK_TILE == 0 and seq % ATT_BLOCK == 0 and seq % RET_BLOCK == 0

    cos_t, sin_t = _rotary_tables(seq)
    decay, xi, zeta, gamma = _retention_tables()
    row = lambda v: v.reshape(1, -1)

    xf = x.reshape(batch * seq, d_model)
    for l in range(depth):
        att, cu, ret, gate = _in_proj(xf, row(norm_mix_pre[l]), w_in[l].astype(BF16), cos_t, sin_t)
        y_att = _attention(att, _attention_bias(attn_rel_bias[l]), batch)
        y_conv = _conv(cu, conv_w[l], row(conv_b[l]), row(conv_ln_g[l]), row(conv_ln_b[l]), seq)
        y_ret = _retention(ret, gate, decay, xi, zeta, gamma, batch)
        xf = _out_mlp(xf, y_att, y_conv, y_ret, w_out[l].astype(BF16), w_up[l].astype(BF16),
                      w_down[l].astype(BF16), row(norm_mix_post[l]), row(norm_mlp_pre[l]),
                      row(norm_mlp_post[l]))
    return xf.reshape(batch, seq, d_model)
```

```python
import functools

import jax
import jax.numpy as jnp
import numpy as np
from jax import lax
from jax.experimental import pallas as pl
from jax.experimental.pallas import tpu as pltpu

D_MODEL = 1024
CHUNK = 64
HEAD_DIM = 64
HALF = HEAD_DIM // 2
ATT_HEADS = 6
ATT_WIDTH = ATT_HEADS * HEAD_DIM
ATT_LEFT_CHUNKS = 8
ATT_BAND = (ATT_LEFT_CHUNKS + 1) * CHUNK
REL_CLIP = 128
CONV_WIDTH = D_MODEL // 4
CONV_KERNEL = 31
RET_HEADS = 6
RET_WIDTH = RET_HEADS * HEAD_DIM
D_MIX = ATT_WIDTH + CONV_WIDTH + RET_WIDTH
D_FF = 4 * D_MODEL
ROPE_BASE = 10000.0
EPS = 1e-6
NEG_INF = -1e30
Q_SCALE = HEAD_DIM ** -0.5

COL_ATT = 0
COL_CONV = 3 * ATT_WIDTH
COL_RET = COL_CONV + 2 * CONV_WIDTH
COL_GATE = COL_RET + 3 * RET_WIDTH
D_IN = COL_GATE + RET_WIDTH

LANES = 128
SUBLANES = 8
VMEM_LIMIT = 56 * 1024 * 1024

TOK_TILE = 512
ATT_BLOCK = ATT_LEFT_CHUNKS * CHUNK
PAIR = 2 * CHUNK
PAIR_BAND = ATT_BAND + CHUNK
BIAS_PERIOD = PAIR_BAND + PAIR
BIAS_VARIANTS = 1 + ATT_BLOCK // PAIR
RET_BLOCK = 256
CONV_HALO = 32
CONV_ROWS = 64
FF_CHUNK = 1024

F32 = jnp.float32
BF16 = jnp.bfloat16


def _rms_norm(x, g):
    return x * lax.rsqrt(jnp.mean(x * x, axis=-1, keepdims=True) + EPS) * g


def _in_proj_kernel(x_ref, g_ref, w_ref, cos_ref, sin_ref, att_ref, conv_ref, ret_ref, gate_ref):
    h = _rms_norm(x_ref[...], g_ref[...]).astype(BF16)

    ua = jnp.dot(h, w_ref[:, COL_ATT:COL_CONV], preferred_element_type=F32)
    att_ref[:, :ATT_WIDTH] = (ua[:, :ATT_WIDTH] * Q_SCALE).astype(BF16)
    att_ref[:, ATT_WIDTH:] = ua[:, ATT_WIDTH:].astype(BF16)

    conv_ref[...] = jnp.dot(h, w_ref[:, COL_CONV:COL_RET], preferred_element_type=F32)

    ur = jnp.dot(h, w_ref[:, COL_RET:COL_GATE], preferred_element_type=F32)
    cos = cos_ref[...]
    sin = sin_ref[...]
    lane = lax.broadcasted_iota(jnp.int32, cos.shape, 1)
    first_half = (lane % HEAD_DIM) < HALF
    for blk in range(2 * RET_WIDTH // LANES):
        xb = ur[:, blk * LANES:(blk + 1) * LANES]
        partner = jnp.where(first_half,
                            pltpu.roll(xb, LANES - HALF, 1),
                            pltpu.roll(xb, HALF, 1))
        rot = xb * cos + partner * sin
        if blk * LANES >= RET_WIDTH:
            rot = rot * Q_SCALE
        ret_ref[:, blk * LANES:(blk + 1) * LANES] = rot.astype(BF16)
    ret_ref[:, 2 * RET_WIDTH:] = ur[:, 2 * RET_WIDTH:].astype(BF16)

    gate_ref[...] = jnp.dot(h, w_ref[:, COL_GATE:], preferred_element_type=F32)


def _in_proj(x, g, w, cos_t, sin_t):
    n = x.shape[0]
    seq_tiles = cos_t.shape[0] // TOK_TILE
    row = lambda i: (i, 0)
    const = lambda i: (0, 0)
    return pl.pallas_call(
        _in_proj_kernel,
        grid=(n // TOK_TILE,),
        in_specs=[
            pl.BlockSpec((TOK_TILE, D_MODEL), row),
            pl.BlockSpec((1, D_MODEL), const),
            pl.BlockSpec((D_MODEL, D_IN), const),
            pl.BlockSpec((TOK_TILE, LANES), lambda i: (i % seq_tiles, 0)),
            pl.BlockSpec((TOK_TILE, LANES), lambda i: (i % seq_tiles, 0)),
        ],
        out_specs=[
            pl.BlockSpec((TOK_TILE, 3 * ATT_WIDTH), row),
            pl.BlockSpec((TOK_TILE, 2 * CONV_WIDTH), row),
            pl.BlockSpec((TOK_TILE, 3 * RET_WIDTH), row),
            pl.BlockSpec((TOK_TILE, RET_WIDTH), row),
        ],
        out_shape=[
            jax.ShapeDtypeStruct((n, 3 * ATT_WIDTH), BF16),
            jax.ShapeDtypeStruct((n, 2 * CONV_WIDTH), F32),
            jax.ShapeDtypeStruct((n, 3 * RET_WIDTH), BF16),
            jax.ShapeDtypeStruct((n, RET_WIDTH), F32),
        ],
        compiler_params=pltpu.CompilerParams(
            dimension_semantics=("arbitrary",), vmem_limit_bytes=VMEM_LIMIT),
        name="in_proj",
    )(x, g, w, cos_t, sin_t)


def _build_bias_table(g_ref, bias_ref):
    rows = lax.broadcasted_iota(jnp.int32, (PAIR, PAIR), 0)
    in_low_chunk = lax.broadcasted_iota(jnp.int32, (PAIR, PAIR), 1) < CHUNK
    masked = jnp.full((PAIR, PAIR), NEG_INF, F32)
    n_prev = [0] + [ATT_BLOCK - pr * PAIR for pr in range(ATT_BLOCK // PAIR)]
    for h in range(ATT_HEADS):
        periodic = jnp.broadcast_to(g_ref[h:h + 1, :], (PAIR, BIAS_PERIOD))
        for r0 in range(0, PAIR_BAND, PAIR):
            tile = pltpu.roll(periodic, r0, 1, stride=1, stride_axis=0)[:, :PAIR]
            own_band = ((in_low_chunk & (rows + r0 < ATT_BAND))
                        | (~in_low_chunk & (rows + r0 >= CHUNK)))
            tile = jnp.where(own_band, tile, NEG_INF)
            for variant, n in enumerate(n_prev):
                bias_ref[variant * ATT_HEADS + h, r0:r0 + PAIR, :] = tile if r0 >= n else masked


def _attn_kernel(q_ref, kp_ref, kc_ref, vp_ref, vc_ref, g_ref, o_ref, v_t, s_buf, bias_ref):
    @pl.when((pl.program_id(0) == 0) & (pl.program_id(1) == 0))
    def _():
        _build_bias_table(g_ref, bias_ref)

    first_block = pl.program_id(1) == 0
    v_t[:, :ATT_BLOCK] = vp_ref[...].astype(F32).T.astype(BF16)
    v_t[:, ATT_BLOCK:] = vc_ref[...].astype(F32).T.astype(BF16)
    lane = lax.broadcasted_iota(jnp.int32, (PAIR, LANES), 1)
    low_head = lane < HEAD_DIM

    head_pairs = ATT_HEADS // 2
    n_units = (ATT_BLOCK // PAIR) * head_pairs

    def scores(u):
        pr, hp = divmod(u, head_pairs)
        row0 = pr * PAIR
        cols = slice(hp * LANES, (hp + 1) * LANES)
        variant = jnp.where(first_block, (1 + pr) * ATT_HEADS, 0)
        q2 = q_ref[row0:row0 + PAIR, cols]
        qz = jnp.concatenate([jnp.where(low_head, q2, jnp.zeros_like(q2)),
                              jnp.where(low_head, jnp.zeros_like(q2), q2)], axis=0)
        kb = jnp.concatenate([kp_ref[row0:, cols], kc_ref[:row0 + PAIR, cols]], axis=0)
        s_pair = lax.dot_general(kb, qz, (((1,), (1,)), ((), ())),
                                 preferred_element_type=F32)
        maxima = []
        for hh in range(2):
            s = s_pair[:, hh * PAIR:(hh + 1) * PAIR] + bias_ref[variant + 2 * hp + hh]
            s_buf[u % 2, :, hh * PAIR:(hh + 1) * PAIR] = s
            maxima.append(jnp.max(s, axis=0, keepdims=True))
        return maxima

    def outputs(u, maxima):
        pr, hp = divmod(u, head_pairs)
        row0 = pr * PAIR
        tiles = []
        for hh in range(2):
            h = 2 * hp + hh
            p = jnp.exp(s_buf[u % 2, :, hh * PAIR:(hh + 1) * PAIR] - maxima[hh])
            denom = jnp.sum(p, axis=0, keepdims=True)
            o_t = jnp.dot(v_t[h * HEAD_DIM:(h + 1) * HEAD_DIM, row0:row0 + PAIR_BAND],
                          p.astype(BF16), preferred_element_type=F32)
            tiles.append(o_t * (1.0 / denom))
        return tiles

    maxima = scores(0)
    out_t = []
    for u in range(n_units):
        next_maxima = scores(u + 1) if u + 1 < n_units else None
        out_t += outputs(u, maxima)
        maxima = next_maxima
        if (u + 1) % head_pairs == 0:
            row0 = (u // head_pairs) * PAIR
            o_ref[row0:row0 + PAIR, :] = jnp.concatenate(out_t, axis=0).T.astype(o_ref.dtype)
            out_t = []


def _attention(att, bias_rows, batch):
    n = att.shape[0]
    nb = n // batch // ATT_BLOCK
    cur = lambda col: (lambda b, i: (b * nb + i, col))
    prev = lambda col: (lambda b, i: (b * nb + jnp.maximum(i - 1, 0), col))
    blk = (ATT_BLOCK, ATT_WIDTH)
    return pl.pallas_call(
        _attn_kernel,
        grid=(batch, nb),
        in_specs=[
            pl.BlockSpec(blk, cur(0)),
            pl.BlockSpec(blk, prev(1)),
            pl.BlockSpec(blk, cur(1)),
            pl.BlockSpec(blk, prev(2)),
            pl.BlockSpec(blk, cur(2)),
            pl.BlockSpec((ATT_HEADS, BIAS_PERIOD), lambda b, i: (0, 0)),
        ],
        out_specs=pl.BlockSpec(blk, cur(0)),
        out_shape=jax.ShapeDtypeStruct((n, ATT_WIDTH), BF16),
        scratch_shapes=[pltpu.VMEM((ATT_WIDTH, 2 * ATT_BLOCK), BF16),
                        pltpu.VMEM((2, PAIR_BAND, 2 * PAIR), F32),
                        pltpu.VMEM((BIAS_VARIANTS * ATT_HEADS, PAIR_BAND, PAIR), F32)],
        compiler_params=pltpu.CompilerParams(
            dimension_semantics=("arbitrary", "arbitrary"), vmem_limit_bytes=VMEM_LIMIT),
        name="attention",
    )(att, att, att, att, att, bias_rows)


def _conv_kernel(halo_ref, cur_ref, w_ref, b_ref, lng_ref, lnb_ref, o_ref, ybuf, *, seq_tiles):
    first_tile = (pl.program_id(0) % seq_tiles) == 0
    halo = halo_ref[...]
    y_halo = halo[:, :CONV_WIDTH] * jax.nn.sigmoid(halo[:, CONV_WIDTH:])
    ybuf[0, :CONV_HALO] = jnp.where(first_tile, 0.0, y_halo)
    cur = cur_ref[...]
    ybuf[0, CONV_HALO:] = cur[:, :CONV_WIDTH] * jax.nn.sigmoid(cur[:, CONV_WIDTH:])
    lead = CONV_HALO - (CONV_KERNEL - 1)
    shifted_rows = CONV_HALO + TOK_TILE - SUBLANES
    for r in range(1, SUBLANES):
        ybuf[r, :shifted_rows] = ybuf[0, r:r + shifted_rows]

    for row0 in range(0, TOK_TILE, CONV_ROWS):
        acc = jnp.zeros((CONV_ROWS, CONV_WIDTH), F32)
        for j in range(CONV_KERNEL):
            r = (lead + j) % SUBLANES
            start = row0 + lead + j - r
            acc = acc + ybuf[r, start:start + CONV_ROWS] * w_ref[j:j + 1, :]
        y = acc + b_ref[...]
        mu = jnp.mean(y, axis=-1, keepdims=True)
        var = jnp.mean(jnp.square(y - mu), axis=-1, keepdims=True)
        y = (y - mu) * lax.rsqrt(var + EPS) * lng_ref[...] + lnb_ref[...]
        o_ref[row0:row0 + CONV_ROWS, :] = (y * jax.nn.sigmoid(y)).astype(o_ref.dtype)


def _conv(cu, w, b, ln_g, ln_b, seq):
    n = cu.shape[0]
    halo_per_tile = TOK_TILE // CONV_HALO
    const = lambda i: (0, 0)
    return pl.pallas_call(
        functools.partial(_conv_kernel, seq_tiles=seq // TOK_TILE),
        grid=(n // TOK_TILE,),
        in_specs=[
            pl.BlockSpec((CONV_HALO, 2 * CONV_WIDTH),
                         lambda i: (jnp.maximum(i * halo_per_tile - 1, 0), 0)),
            pl.BlockSpec((TOK_TILE, 2 * CONV_WIDTH), lambda i: (i, 0)),
            pl.BlockSpec((CONV_KERNEL, CONV_WIDTH), const),
            pl.BlockSpec((1, CONV_WIDTH), const),
            pl.BlockSpec((1, CONV_WIDTH), const),
            pl.BlockSpec((1, CONV_WIDTH), const),
        ],
        out_specs=pl.BlockSpec((TOK_TILE, CONV_WIDTH), lambda i: (i, 0)),
        out_shape=jax.ShapeDtypeStruct((n, CONV_WIDTH), BF16),
        scratch_shapes=[pltpu.VMEM((SUBLANES, CONV_HALO + TOK_TILE, CONV_WIDTH), F32)],
        compiler_params=pltpu.CompilerParams(
            dimension_semantics=("arbitrary",), vmem_limit_bytes=VMEM_LIMIT),
        name="conv",
    )(cu, cu, w, b, ln_g, ln_b)


def _ret_kernel(q_ref, k_ref, v_ref, g_ref, decay_ref, xi_ref, zeta_ref, gamma_ref, diag_ref,
                o_ref, state):
    @pl.when(pl.program_id(1) == 0)
    def _():
        state[...] = jnp.zeros_like(state)

    v_t = v_ref[...].astype(F32).T.astype(BF16)
    kz = (k_ref[...].astype(F32) * zeta_ref[...]).astype(BF16)
    lane = lax.broadcasted_iota(jnp.int32, (RET_BLOCK, LANES), 1)
    low_head = lane < HEAD_DIM
    nt = (((1,), (1,)), ((), ()))
    pairs = RET_HEADS // 2

    def decayed_scores(p):
        cols = slice(p * LANES, (p + 1) * LANES)
        q2 = q_ref[:, cols]
        qz = jnp.concatenate([jnp.where(low_head, q2, jnp.zeros_like(q2)),
                              jnp.where(low_head, jnp.zeros_like(q2), q2)], axis=0)
        sc = lax.dot_general(k_ref[:, cols], qz, nt, preferred_element_type=F32)
        return (sc * decay_ref[p]).astype(BF16)

    y_t = []
    sc_next = decayed_scores(0)
    for p in range(pairs):
        cols = slice(p * LANES, (p + 1) * LANES)
        q2 = q_ref[:, cols]
        v2_t = v_t[p * LANES:(p + 1) * LANES, :]
        sc_t = sc_next
        if p + 1 < pairs:
            sc_next = decayed_scores(p + 1)
        intra = jnp.concatenate(
            [jnp.dot(v2_t[hh * HEAD_DIM:(hh + 1) * HEAD_DIM, :],
                     sc_t[:, hh * RET_BLOCK:(hh + 1) * RET_BLOCK], preferred_element_type=F32)
             for hh in range(2)], axis=0)
        st = state[p]
        cross = lax.dot_general(st.astype(BF16), q2, nt, preferred_element_type=F32)
        y = intra + cross * xi_ref[p]
        kv_t = jnp.dot(v2_t, kz[:, cols], preferred_element_type=F32)
        state[p] = st * gamma_ref[p] + kv_t * diag_ref[...]
        for hh in range(2):
            yh = y[hh * HEAD_DIM:(hh + 1) * HEAD_DIM, :]
            mu = jnp.mean(yh, axis=0, keepdims=True)
            var = jnp.mean(jnp.square(yh - mu), axis=0, keepdims=True)
            y_t.append((yh - mu) * lax.rsqrt(var + EPS))
    yn = jnp.concatenate(y_t, axis=0).T
    gate = g_ref[...]
    o_ref[...] = (gate * jax.nn.sigmoid(gate) * yn).astype(o_ref.dtype)


def _retention(ret, gate, tables, batch):
    n = ret.shape[0]
    nb = n // batch // RET_BLOCK
    blk = (RET_BLOCK, RET_WIDTH)
    col = lambda c: (lambda b, t: (b * nb + t, c))
    whole = lambda a: pl.BlockSpec(a.shape, lambda b, t: (0,) * a.ndim)
    pairs = RET_HEADS // 2
    return pl.pallas_call(
        _ret_kernel,
        grid=(batch, nb),
        in_specs=[
            pl.BlockSpec(blk, col(0)),
            pl.BlockSpec(blk, col(1)),
            pl.BlockSpec(blk, col(2)),
            pl.BlockSpec(blk, col(0)),
        ] + [whole(t) for t in tables],
        out_specs=pl.BlockSpec(blk, col(0)),
        out_shape=jax.ShapeDtypeStruct((n, RET_WIDTH), BF16),
        scratch_shapes=[pltpu.VMEM((pairs, LANES, LANES), F32)],
        compiler_params=pltpu.CompilerParams(
            dimension_semantics=("arbitrary", "arbitrary"), vmem_limit_bytes=VMEM_LIMIT),
        name="retention",
    )(ret, ret, ret, gate, *tables)


def _out_mlp_kernel(x_ref, ya_ref, yc_ref, yr_ref, wo_ref, wu_ref, wd_ref,
                    g_post_ref, g_pre2_ref, g_post2_ref, o_ref, mix_in, hid, acc):
    mix_in[:, :ATT_WIDTH] = ya_ref[...]
    mix_in[:, ATT_WIDTH:ATT_WIDTH + CONV_WIDTH] = yc_ref[...]
    mix_in[:, ATT_WIDTH + CONV_WIDTH:] = yr_ref[...]
    mix = jnp.dot(mix_in[...], wo_ref[...], preferred_element_type=F32)
    x1 = x_ref[...] + _rms_norm(mix, g_post_ref[...])
    hid[...] = _rms_norm(x1, g_pre2_ref[...]).astype(BF16)
    for c in range(D_FF // FF_CHUNK):
        cols = slice(c * FF_CHUNK, (c + 1) * FF_CHUNK)
        up = jnp.dot(hid[...], wu_ref[:, cols], preferred_element_type=F32)
        act = jnp.square(jnp.maximum(up, 0.0)).astype(BF16)
        part = jnp.dot(act, wd_ref[cols, :], preferred_element_type=F32)
        if c == 0:
            acc[...] = part
        else:
            acc[...] += part
    o_ref[...] = x1 + _rms_norm(acc[...], g_post2_ref[...])


def _out_mlp(x, y_att, y_conv, y_ret, w_out, w_up, w_down, g_post, g_pre2, g_post2):
    n = x.shape[0]
    row = lambda i: (i, 0)
    const = lambda i: (0, 0)
    resident = lambda shape: pl.BlockSpec(shape, const, pipeline_mode=pl.Buffered(1))
    gain = pl.BlockSpec((1, D_MODEL), const)
    return pl.pallas_call(
        _out_mlp_kernel,
        grid=(n // TOK_TILE,),
        in_specs=[
            pl.BlockSpec((TOK_TILE, D_MODEL), row),
            pl.BlockSpec((TOK_TILE, ATT_WIDTH), row),
            pl.BlockSpec((TOK_TILE, CONV_WIDTH), row),
            pl.BlockSpec((TOK_TILE, RET_WIDTH), row),
            resident((D_MIX, D_MODEL)),
            resident((D_MODEL, D_FF)),
            resident((D_FF, D_MODEL)),
            gain, gain, gain,
        ],
        out_specs=pl.BlockSpec((TOK_TILE, D_MODEL), row),
        out_shape=jax.ShapeDtypeStruct((n, D_MODEL), F32),
        scratch_shapes=[pltpu.VMEM((TOK_TILE, D_MIX), BF16),
                        pltpu.VMEM((TOK_TILE, D_MODEL), BF16),
                        pltpu.VMEM((TOK_TILE, D_MODEL), F32)],
        compiler_params=pltpu.CompilerParams(
            dimension_semantics=("arbitrary",), vmem_limit_bytes=VMEM_LIMIT),
        name="out_mlp",
    )(x, y_att, y_conv, y_ret, w_out, w_up, w_down, g_post, g_pre2, g_post2)


def _rotary_tables(seq):
    inv = 1.0 / (ROPE_BASE ** jnp.linspace(0.0, 1.0, HALF, dtype=F32))
    ang = jnp.arange(seq, dtype=F32)[:, None] * inv[None, :]
    cos = jnp.cos(ang)
    sin = jnp.sin(ang)
    reps = LANES // HEAD_DIM
    cos_t = jnp.tile(jnp.concatenate([cos, cos], axis=1), (1, reps))
    sin_t = jnp.tile(jnp.concatenate([-sin, sin], axis=1), (1, reps))
    return cos_t, sin_t


def _retention_tables():
    pairs = RET_HEADS // 2
    log_g = jnp.log1p(-(2.0 ** (-5.0 - jnp.arange(RET_HEADS, dtype=F32))))
    idx = jnp.arange(RET_BLOCK, dtype=F32)
    diff = idx[None, :] - idx[:, None]
    decay_t = jnp.where(diff >= 0, jnp.exp(jnp.maximum(diff, 0.0) * log_g[:, None, None]), 0.0)
    decay_t = decay_t.reshape(pairs, 2, RET_BLOCK, RET_BLOCK).transpose(0, 2, 1, 3)
    decay_t = decay_t.reshape(pairs, RET_BLOCK, 2 * RET_BLOCK)
    xi = jnp.exp((idx + 1.0)[None, :] * log_g[:, None])
    xi = jnp.repeat(xi, HEAD_DIM, axis=0).reshape(pairs, LANES, RET_BLOCK)
    zeta = jnp.exp((RET_BLOCK - 1.0 - idx)[None, :] * log_g[:, None])
    zeta = jnp.repeat(zeta.T, HEAD_DIM, axis=1)
    gamma = jnp.repeat(jnp.exp(RET_BLOCK * log_g), HEAD_DIM).reshape(pairs, LANES, 1)
    gamma = jnp.broadcast_to(gamma, (pairs, LANES, LANES))
    head_of = np.arange(LANES) // HEAD_DIM
    diag = jnp.asarray(head_of[:, None] == head_of[None, :], F32)
    return decay_t, xi, zeta, gamma, diag


def _attention_bias_rows(rel_bias):
    far = rel_bias[:, 2 * REL_CLIP:]
    lead = BIAS_PERIOD - PAIR_BAND
    tail = BIAS_PERIOD - lead - 2 * REL_CLIP
    return jnp.concatenate([jnp.broadcast_to(far, (ATT_HEADS, lead)), rel_bias[:, :2 * REL_CLIP],
                            jnp.broadcast_to(far, (ATT_HEADS, tail))], axis=1)


def kernel(x, norm_mix_pre, norm_mix_post, norm_mlp_pre, norm_mlp_post, w_in, attn_rel_bias,
           conv_w, conv_b, conv_ln_g, conv_ln_b, w_out, w_up, w_down):
    batch, seq, d_model = x.shape
    depth = w_in.shape[0]
    assert d_model == D_MODEL and w_in.shape[2] == D_IN
    assert seq % TOK_TILE == 0 and seq % ATT_BLOCK == 0 and seq % RET_BLOCK == 0

    cos_t, sin_t = _rotary_tables(seq)
    ret_tables = _retention_tables()
    row = lambda v: v.reshape(1, -1)

    xf = x.reshape(batch * seq, d_model)
    for l in range(depth):
        att, cu, ret, gate = _in_proj(xf, row(norm_mix_pre[l]), w_in[l].astype(BF16), cos_t, sin_t)
        y_att = _attention(att, _attention_bias_rows(attn_rel_bias[l]), batch)
        y_conv = _conv(cu, conv_w[l], row(conv_b[l]), row(conv_ln_g[l]), row(conv_ln_b[l]), seq)
        y_ret = _retention(ret, gate, ret_tables, batch)
        xf = _out_mlp(xf, y_att, y_conv, y_ret, w_out[l].astype(BF16), w_up[l].astype(BF16),
                      w_down[l].astype(BF16), row(norm_mix_post[l]), row(norm_mlp_pre[l]),
                      row(norm_mlp_post[l]))
    return xf.reshape(batch, seq, d_model)
```
